```python
import jax, jax.numpy as jnp
from jax import lax
import numpy as np

D_MODEL = 1024
BATCH = 8
SEQ = 8192
DEPTH = 1
DEC_BATCH = 128
DEC_SEQ = 8
PAST_LEN = 8192
PAGE_SIZE = 128

HEAD_DIM = 64
MIX_WIDTH = D_MODEL
N_HEADS_FOX = MIX_WIDTH // (2 * HEAD_DIM)
N_HEADS_MOBA = MIX_WIDTH // (2 * HEAD_DIM)
FOX_WIDTH = N_HEADS_FOX * HEAD_DIM
MOBA_WIDTH = N_HEADS_MOBA * HEAD_DIM
IN_COLS = 3 * FOX_WIDTH + N_HEADS_FOX + 3 * MOBA_WIDTH + MIX_WIDTH
ROT_DIM = HEAD_DIM // 4
ROPE_THETA = 500000.0
MOBA_BLOCK = 256
MOBA_TOPK = 3
Q_BLOCK = 128
MOBA_Q_BLOCK = 64
FORGET_BIAS_INIT = 2.0
NORM_EPS = 1e-6
SCALE = HEAD_DIM ** -0.5
N_PAGES = PAST_LEN // PAGE_SIZE
N_POOL = (DEC_BATCH * N_PAGES * 5 + 3) // 4

kernel_name = "hymba_fox_moba_gated_step"


def rmsnorm(x, g):
    xf = x.astype(jnp.float32)
    y = xf * lax.rsqrt(jnp.mean(xf * xf, axis=-1, keepdims=True) + NORM_EPS)
    return (y * g.astype(jnp.float32)).astype(x.dtype)


def partial_rope(x, pos):
    half = ROT_DIM // 2
    inv_freq = ROPE_THETA ** (-(jnp.arange(half, dtype=jnp.float32) * 2.0 / ROT_DIM))
    ang = pos.astype(jnp.float32)[:, None] * inv_freq[None, :]
    cos = jnp.cos(ang)[None, :, None, :]
    sin = jnp.sin(ang)[None, :, None, :]
    xf = x.astype(jnp.float32)
    x1 = xf[..., :half]
    x2 = xf[..., half:ROT_DIM]
    out = jnp.concatenate([x1 * cos - x2 * sin, x2 * cos + x1 * sin, xf[..., ROT_DIM:]], axis=-1)
    return out.astype(x.dtype)


def project(h, w_in, b_forget, pos):
    B, T, _ = h.shape
    z = h @ w_in
    sizes = [FOX_WIDTH] * 3 + [N_HEADS_FOX] + [MOBA_WIDTH] * 3 + [MIX_WIDTH]
    points = [int(v) for v in np.cumsum(sizes)[:-1]]
    parts = jnp.split(z, points, axis=-1)
    qf, kf, vf = [p.reshape(B, T, N_HEADS_FOX, HEAD_DIM) for p in parts[0:3]]
    logf = jax.nn.log_sigmoid((parts[3] + b_forget).astype(jnp.float32))
    qm, km, vm = [p.reshape(B, T, N_HEADS_MOBA, HEAD_DIM) for p in parts[4:7]]
    qm = partial_rope(qm, pos)
    km = partial_rope(km, pos)
    gate = parts[7]
    return qf, kf, vf, logf, qm, km, vm, gate


def fox_attend(q, c_q, qpos, k, v, c_k):
    s = jnp.einsum('bqhd,bkhd->bhqk', q, k, preferred_element_type=jnp.float32) * SCALE
    s = s + (jnp.transpose(c_q, (0, 2, 1))[..., :, None] - jnp.transpose(c_k, (0, 2, 1))[..., None, :])
    kpos = jnp.arange(k.shape[1])
    causal = kpos[None, :] <= qpos[:, None]
    s = jnp.where(causal, s, -jnp.inf)
    p = jax.nn.softmax(s, axis=-1)
    return jnp.einsum('bhqk,bkhd->bqhd', p.astype(v.dtype), v)


def fox_prompt(q, k, v, logf):
    B, S, H, Dh = q.shape
    c = jnp.cumsum(logf, axis=1)

    def one(i):
        start = i * Q_BLOCK
        qb = lax.dynamic_slice_in_dim(q, start, Q_BLOCK, axis=1)
        cb = lax.dynamic_slice_in_dim(c, start, Q_BLOCK, axis=1)
        qpos = start + jnp.arange(Q_BLOCK)
        return fox_attend(qb, cb, qpos, k, v, c)

    o = lax.map(one, jnp.arange(S // Q_BLOCK))
    return jnp.moveaxis(o, 0, 1).reshape(B, S, H, Dh)


def moba_blocks(k, v):
    B, L, H, Dh = k.shape
    nbp = max(-(-L // MOBA_BLOCK), MOBA_TOPK)
    pad = nbp * MOBA_BLOCK - L
    kp = jnp.pad(k, ((0, 0), (0, pad), (0, 0), (0, 0))).reshape(B, nbp, MOBA_BLOCK, H, Dh)
    vp = jnp.pad(v, ((0, 0), (0, pad), (0, 0), (0, 0))).reshape(B, nbp, MOBA_BLOCK, H, Dh)
    kmean = jnp.mean(kp.astype(jnp.float32), axis=2)
    return jnp.transpose(kp, (0, 3, 1, 2, 4)), jnp.transpose(vp, (0, 3, 1, 2, 4)), kmean


def moba_attend(q, qpos, kb, vb, kmean):
    B, Tq, H, Dh = q.shape
    nbp = kmean.shape[1]
    blk = qpos // MOBA_BLOCK
    gs = jnp.einsum('bqhd,bnhd->bqhn', q.astype(jnp.float32), kmean)
    fully_past = jnp.arange(nbp)[None, :] < blk[:, None]
    gs = jnp.where(fully_past[None, :, None, :], gs, -jnp.inf)
    _, top = lax.top_k(gs, MOBA_TOPK)
    own = jnp.broadcast_to(blk[None, :, None, None], (B, Tq, H, 1)).astype(top.dtype)
    sel = jnp.concatenate([top, own], axis=-1)
    bi = jnp.arange(B)[:, None, None, None]
    hi = jnp.arange(H)[None, None, :, None]
    ks = kb[bi, hi, sel]
    vs = vb[bi, hi, sel]
    s = jnp.einsum('bqhd,bqhjrd->bqhjr', q, ks, preferred_element_type=jnp.float32) * SCALE
    kpos = sel[..., None] * MOBA_BLOCK + jnp.arange(MOBA_BLOCK)
    slot = jnp.arange(MOBA_TOPK + 1)
    past_ok = slot[None, :] < jnp.minimum(blk, MOBA_TOPK)[:, None]
    is_own = (slot == MOBA_TOPK)[None, None, None, :, None]
    own_ok = is_own & (kpos <= qpos[None, :, None, None, None])
    valid = past_ok[None, :, None, :, None] | own_ok
    s = jnp.where(valid, s, -jnp.inf)
    p = jax.nn.softmax(s.reshape(B, Tq, H, -1), axis=-1).reshape(s.shape)
    return jnp.einsum('bqhjr,bqhjrd->bqhd', p.astype(vs.dtype), vs)


def moba_prompt(q, k, v):
    B, S, H, Dh = q.shape
    kb, vb, kmean = moba_blocks(k, v)

    def one(i):
        start = i * MOBA_Q_BLOCK
        qb = lax.dynamic_slice_in_dim(q, start, MOBA_Q_BLOCK, axis=1)
        qpos = start + jnp.arange(MOBA_Q_BLOCK)
        return moba_attend(qb, qpos, kb, vb, kmean)

    o = lax.map(one, jnp.arange(S // MOBA_Q_BLOCK))
    return jnp.moveaxis(o, 0, 1).reshape(B, S, H, Dh)


def merge(of, om, gate, w_out):
    B, T = of.shape[:2]
    mixed = jnp.concatenate([of.reshape(B, T, FOX_WIDTH), om.reshape(B, T, MOBA_WIDTH)], axis=-1)
    return (mixed * jax.nn.silu(gate)) @ w_out


def gather_pages(pool, page_table):
    g = pool[page_table]
    return g.reshape((page_table.shape[0], page_table.shape[1] * pool.shape[1]) + pool.shape[2:])


def setup_inputs(seed: int = 0) -> dict:
    key = jax.random.key(seed)
    ks = jax.random.split(key, 14)
    f32 = jnp.float32
    x_prompt = jax.random.normal(ks[0], (BATCH, SEQ, D_MODEL), f32)
    x_sample = jax.random.normal(ks[1], (DEC_BATCH, DEC_SEQ, D_MODEL), f32)
    kv_shape_f = (DEPTH, N_POOL, PAGE_SIZE, N_HEADS_FOX, HEAD_DIM)
    kv_shape_m = (DEPTH, N_POOL, PAGE_SIZE, N_HEADS_MOBA, HEAD_DIM)
    cache_fox_k = jax.random.normal(ks[2], kv_shape_f, f32)
    cache_fox_v = jax.random.normal(ks[3], kv_shape_f, f32)
    cache_fox_logf = jax.nn.log_sigmoid(FORGET_BIAS_INIT + jax.random.normal(ks[4], (DEPTH, N_POOL, PAGE_SIZE, N_HEADS_FOX), f32))
    cache_moba_k = jax.random.normal(ks[5], kv_shape_m, f32)
    cache_moba_v = jax.random.normal(ks[6], kv_shape_m, f32)
    perm = jax.random.permutation(ks[7], N_POOL)[: DEC_BATCH * N_PAGES]
    page_table = perm.reshape(DEC_BATCH, N_PAGES).astype(jnp.int32)
    norm_gain = 1.0 + 0.02 * jax.random.normal(ks[8], (DEPTH, D_MODEL), f32)
    w_in = jax.random.normal(ks[9], (DEPTH, D_MODEL, IN_COLS), f32) * D_MODEL ** -0.5
    b_forget = FORGET_BIAS_INIT + 0.1 * jax.random.normal(ks[10], (DEPTH, N_HEADS_FOX), f32)
    w_out = jax.random.normal(ks[11], (DEPTH, MIX_WIDTH, D_MODEL), f32) * MIX_WIDTH ** -0.5
    final_norm_gain = 1.0 + 0.02 * jax.random.normal(ks[12], (D_MODEL,), f32)
    return {"x_prompt": x_prompt, "x_sample": x_sample,
            "cache_fox_k": cache_fox_k, "cache_fox_v": cache_fox_v, "cache_fox_logf": cache_fox_logf,
            "cache_moba_k": cache_moba_k, "cache_moba_v": cache_moba_v,
            "page_table": page_table, "norm_gain": norm_gain, "w_in": w_in,
            "b_forget": b_forget, "w_out": w_out, "final_norm_gain": final_norm_gain}


def reference(x_prompt, x_sample, cache_fox_k, cache_fox_v, cache_fox_logf, cache_moba_k,
              cache_moba_v, page_table, norm_gain, w_in, b_forget, w_out, final_norm_gain):
    S = x_prompt.shape[1]
    T = x_sample.shape[1]
    past_len = page_table.shape[1] * PAGE_SIZE
    pos_p = jnp.arange(S)
    pos_s = past_len + jnp.arange(T)
    hp = x_prompt
    hs = x_sample
    fk_p, fv_p, fl_p, mk_p, mv_p = [], [], [], [], []
    fk_s, fv_s, fl_s, mk_s, mv_s = [], [], [], [], []
    for l in range(DEPTH):
        qf, kf, vf, logf, qm, km, vm, gate = project(rmsnorm(hp, norm_gain[l]), w_in[l], b_forget[l], pos_p)
        of = fox_prompt(qf, kf, vf, logf)
        om = moba_prompt(qm, km, vm)
        hp = hp + merge(of, om, gate, w_out[l])
        fk_p.append(kf); fv_p.append(vf); fl_p.append(logf); mk_p.append(km); mv_p.append(vm)

        qf2, kf2, vf2, logf2, qm2, km2, vm2, gate2 = project(rmsnorm(hs, norm_gain[l]), w_in[l], b_forget[l], pos_s)
        kf_all = jnp.concatenate([gather_pages(cache_fox_k[l], page_table).astype(kf2.dtype), kf2], axis=1)
        vf_all = jnp.concatenate([gather_pages(cache_fox_v[l], page_table).astype(vf2.dtype), vf2], axis=1)
        c_all = jnp.cumsum(jnp.concatenate(
            [gather_pages(cache_fox_logf[l], page_table).astype(jnp.float32), logf2], axis=1), axis=1)
        of2 = fox_attend(qf2, c_all[:, past_len:], pos_s, kf_all, vf_all, c_all)
        km_all = jnp.concatenate([gather_pages(cache_moba_k[l], page_table).astype(km2.dtype), km2], axis=1)
        vm_all = jnp.concatenate([gather_pages(cache_moba_v[l], page_table).astype(vm2.dtype), vm2], axis=1)
        kb, vb, kmean = moba_blocks(km_all, vm_all)
        om2 = moba_attend(qm2, pos_s, kb, vb, kmean)
        hs = hs + merge(of2, om2, gate2, w_out[l])
        fk_s.append(kf2); fv_s.append(vf2); fl_s.append(logf2); mk_s.append(km2); mv_s.append(vm2)

    y_prompt = rmsnorm(hp, final_norm_gain)
    y_sample = rmsnorm(hs, final_norm_gain)
    return (y_prompt, y_sample,
            jnp.stack(fk_p), jnp.stack(fv_p), jnp.stack(fl_p), jnp.stack(mk_p), jnp.stack(mv_p),
            jnp.stack(fk_s), jnp.stack(fv_s), jnp.stack(fl_s), jnp.stack(mk_s), jnp.stack(mv_s))
```

```python
import functools

import jax
import jax.numpy as jnp
from jax import lax
from jax.experimental import pallas as pl
from jax.experimental.pallas import tpu as pltpu

f32 = jnp.float32
bf16 = jnp.bfloat16

HEAD_DIM = 64
N_HEADS = 8
WIDTH = N_HEADS * HEAD_DIM
ROT_HALF = 8
ROPE_THETA = 500000.0
PAGE = 128
MOBA_BLOCK = 256
MOBA_TOPK = 3
NORM_EPS = 1e-6
LOG2E = 1.4426950408889634
QSCALE = HEAD_DIM ** -0.5 * LOG2E
NEG = -1e30
LANES = 128
F_PAD = 16
VMEM_LIMIT = 56 * 1024 * 1024

OFF_QF, OFF_KF, OFF_VF = 0, WIDTH, 2 * WIDTH
OFF_QM, OFF_KM, OFF_VM = 3 * WIDTH, 4 * WIDTH, 5 * WIDTH
OFF_G = 6 * WIDTH
OFF_F = 6 * WIDTH + 2 * WIDTH
WT_ROWS = OFF_F + F_PAD

_NT = (((1,), (1,)), ((), ()))
_HI = lax.Precision.HIGHEST


def _cparams(sem):
    return pltpu.CompilerParams(dimension_semantics=sem, vmem_limit_bytes=VMEM_LIMIT)


def _log_sigmoid(x):
    return jnp.minimum(x, 0.0) - jnp.log(1.0 + jnp.exp(-jnp.abs(x)))


def _silu(x):
    return x * (1.0 / (1.0 + jnp.exp(-x)))


def _rmsnorm(x, g):
    return x * lax.rsqrt(jnp.mean(x * x, axis=-1, keepdims=True) + NORM_EPS) * g


def _split3(x):
    hi = x.astype(bf16).astype(f32)
    r = x - hi
    mid = r.astype(bf16).astype(f32)
    lo = (r - mid).astype(bf16).astype(f32)
    return hi, mid, lo


def _rope_rows(z, cos, sin):
    t = z.shape[-1]
    z3 = z.reshape(N_HEADS, HEAD_DIM, t)
    x1 = z3[:, 0:ROT_HALF]
    x2 = z3[:, ROT_HALF:2 * ROT_HALF]
    out = jnp.concatenate([x1 * cos - x2 * sin, x2 * cos + x1 * sin, z3[:, 2 * ROT_HALF:]], axis=1)
    return out.reshape(N_HEADS * HEAD_DIM, t)


def _top3_bias(gs, idx, valid_count_axis):
    ax = valid_count_axis
    big = jnp.int32(2 ** 30)
    sel = jnp.zeros(gs.shape, jnp.bool_)
    for _ in range(MOBA_TOPK):
        mx = jnp.max(gs, axis=ax, keepdims=True)
        first = jnp.min(jnp.where(gs == mx, idx, big), axis=ax, keepdims=True)
        pick = jnp.logical_and(idx == first, mx > -jnp.inf)
        sel = jnp.logical_or(sel, pick)
        gs = jnp.where(pick, -jnp.inf, gs)
    return sel


def _proj_prompt_kernel(x_ref, g_ref, wt_ref, bf_ref, cos_ref, sin_ref,
                        qf_ref, kf_ref, vf_ref, vfb_ref, logf_ref, c_ref, kfn_ref, kxf_ref,
                        qm_ref, km_ref, vm_ref, vmb_ref, kmn_ref, sel_ref, gate_ref,
                        carry_ref, kmean_ref, *, tm, nb):
    s = pl.program_id(1)

    @pl.when(s == 0)
    def _():
        carry_ref[...] = jnp.zeros_like(carry_ref)
        kmean_ref[...] = jnp.zeros_like(kmean_ref)

    h = _rmsnorm(x_ref[0], g_ref[...]).astype(bf16)

    def seg(lo, n):
        return lax.dot_general(wt_ref[lo:lo + n, :], h, _NT, preferred_element_type=f32)

    qf_ref[0] = (seg(OFF_QF, WIDTH) * QSCALE).astype(bf16)
    kf = seg(OFF_KF, WIDTH)
    kf_ref[0] = kf
    kfn_ref[0] = kf.T.astype(bf16)
    vf = seg(OFF_VF, WIDTH)
    vf_ref[0] = vf
    vfb_ref[0] = vf.astype(bf16)

    logf = _log_sigmoid(seg(OFF_F, F_PAD)[0:N_HEADS] + bf_ref[...])
    logf_ref[0] = logf
    lane = lax.broadcasted_iota(jnp.int32, logf.shape, 1)
    c = logf
    sh = 1
    while sh < tm:
        c = c + jnp.where(lane >= sh, pltpu.roll(c, sh, 1), 0.0)
        sh *= 2
    c = c + carry_ref[:, 0:1]
    carry_ref[...] = jnp.broadcast_to(c[:, tm - 1:tm], carry_ref.shape)
    c2 = c * LOG2E
    c_ref[0] = c2

    j8 = lax.broadcasted_iota(jnp.int32, (N_HEADS, tm), 0)
    blocks = []
    for hh in range(N_HEADS):
        hi, mid, lo = _split3(-c2[hh:hh + 1, :])
        blocks.append(jnp.where(j8 == 0, hi, jnp.where(j8 == 1, mid, jnp.where(j8 == 2, lo,
                      jnp.where(j8 < 6, 1.0, 0.0)))))
    blocks.append(jnp.zeros((LANES - N_HEADS * 8, tm), f32))
    kxf_ref[0] = jnp.concatenate(blocks, axis=0).T.astype(bf16)

    cos = cos_ref[...]
    sin = sin_ref[...]
    qm = _rope_rows(seg(OFF_QM, WIDTH), cos, sin)
    qm_ref[0] = (qm * QSCALE).astype(bf16)
    km = _rope_rows(seg(OFF_KM, WIDTH), cos, sin)
    km_ref[0] = km
    kmn = km.T
    kmn_ref[0] = kmn.astype(bf16)
    vm = seg(OFF_VM, WIDTH)
    vm_ref[0] = vm
    vmb_ref[0] = vm.astype(bf16)

    nblk_t = tm // MOBA_BLOCK
    for j in range(nblk_t):
        kmean_ref[pl.ds(s * nblk_t + j, 1), :] = jnp.mean(
            kmn[j * MOBA_BLOCK:(j + 1) * MOBA_BLOCK], axis=0, keepdims=True)
    km_all = kmean_ref[...]

    n_i = lax.broadcasted_iota(jnp.int32, (nb, tm), 0)
    blk = (s * tm + lax.broadcasted_iota(jnp.int32, (nb, tm), 1)) // MOBA_BLOCK
    valid = n_i < blk
    own = n_i == blk
    for hh in range(N_HEADS):
        gs = lax.dot_general(km_all[:, hh * HEAD_DIM:(hh + 1) * HEAD_DIM],
                             qm[hh * HEAD_DIM:(hh + 1) * HEAD_DIM],
                             (((1,), (0,)), ((), ())), precision=_HI, preferred_element_type=f32)
        sel = _top3_bias(jnp.where(valid, gs, -jnp.inf), n_i, 0)
        bias = jnp.where(jnp.logical_or(sel, own), 0.0, NEG)
        sel_ref[0, hh * nb:(hh + 1) * nb, :] = bias.astype(bf16)

    gate_ref[0] = _silu(seg(OFF_G, 2 * WIDTH))


def _proj_prompt(x, g, wt, bfc, cos, sin, tm):
    B, S, D = x.shape
    nb = S // MOBA_BLOCK
    tok = lambda rows, dt: jax.ShapeDtypeStruct((B, rows, S), dt)
    nat = lambda cols, dt: jax.ShapeDtypeStruct((B, S, cols), dt)
    tspec = lambda rows: pl.BlockSpec((1, rows, tm), lambda b, s: (b, 0, s))
    nspec = lambda cols: pl.BlockSpec((1, tm, cols), lambda b, s: (b, s, 0))
    out_shape = [tok(WIDTH, bf16), tok(WIDTH, f32), tok(WIDTH, f32), tok(WIDTH, bf16),
                 tok(N_HEADS, f32), tok(N_HEADS, f32), nat(WIDTH, bf16), nat(LANES, bf16),
                 tok(WIDTH, bf16), tok(WIDTH, f32), tok(WIDTH, f32), tok(WIDTH, bf16),
                 nat(WIDTH, bf16), tok(N_HEADS * nb, bf16), tok(2 * WIDTH, f32)]
    out_specs = [tspec(WIDTH), tspec(WIDTH), tspec(WIDTH), tspec(WIDTH),
                 tspec(N_HEADS), tspec(N_HEADS), nspec(WIDTH), nspec(LANES),
                 tspec(WIDTH), tspec(WIDTH), tspec(WIDTH), tspec(WIDTH),
                 nspec(WIDTH), tspec(N_HEADS * nb), tspec(2 * WIDTH)]
    in_specs = [pl.BlockSpec((1, tm, D), lambda b, s: (b, s, 0)),
                pl.BlockSpec((1, D), lambda b, s: (0, 0)),
                pl.BlockSpec((WT_ROWS, D), lambda b, s: (0, 0)),
                pl.BlockSpec((N_HEADS, 1), lambda b, s: (0, 0)),
                pl.BlockSpec((ROT_HALF, tm), lambda b, s: (0, s)),
                pl.BlockSpec((ROT_HALF, tm), lambda b, s: (0, s))]
    return pl.pallas_call(
        functools.partial(_proj_prompt_kernel, tm=tm, nb=nb),
        out_shape=out_shape, grid=(B, S // tm), in_specs=in_specs, out_specs=out_specs,
        scratch_shapes=[pltpu.VMEM((N_HEADS, LANES), f32), pltpu.VMEM((nb, WIDTH), f32)],
        compiler_params=_cparams(("arbitrary", "arbitrary")),
        name="proj_prompt",
    )(x, g, wt, bfc, cos, sin)


def _flash_core(qa_ref, kn_ref, vt_ref, kx_fn, o_ref, q0, tq, tk):
    def step(k0, carry, masked):
        m, l, acc = carry
        ka = jnp.concatenate([kn_ref[0, pl.ds(k0, tk), :], kx_fn(k0)], axis=1)
        st = jnp.dot(ka, qa_ref[...], preferred_element_type=f32)
        if masked:
            kpos = k0 + lax.broadcasted_iota(jnp.int32, (tk, tq), 0)
            qpos = q0 + lax.broadcasted_iota(jnp.int32, (tk, tq), 1)
            st = jnp.where(kpos <= qpos, st, NEG)
        m_new = jnp.maximum(m, jnp.max(st, axis=0, keepdims=True))
        alpha = jnp.exp2(m - m_new)
        p = jnp.exp2(st - m_new)
        l = alpha * l + jnp.sum(p, axis=0, keepdims=True)
        pv = jnp.dot(vt_ref[0, :, pl.ds(k0, tk)], p.astype(bf16), preferred_element_type=f32)
        return m_new, l, alpha * acc + pv

    carry = (jnp.full((1, tq), NEG, f32), jnp.zeros((1, tq), f32), jnp.zeros((HEAD_DIM, tq), f32))
    for d in range(tq // tk):
        carry = step(pl.multiple_of(q0 + d * tk, tk), carry, True)
    carry = lax.fori_loop(0, q0 // tk,
                          lambda i, cr: step(pl.multiple_of(i * tk, tk), cr, False), carry)
    _, l, acc = carry
    o_ref[0] = acc / l


def _q_rows(q, par):
    zero = jnp.zeros_like(q)
    return jnp.concatenate([jnp.where(par == 0, q, zero), jnp.where(par == 1, q, zero)], axis=0)


def _fox_attn_kernel(q_ref, c_ref, kn_ref, kx_ref, vt_ref, o_ref, qa_ref, *, tq, tk):
    hh = pl.program_id(1)
    q0 = pl.multiple_of(pl.program_id(2) * tq, tq)
    qa_ref[0:LANES, :] = _q_rows(q_ref[0], hh % 2)
    hi, mid, lo = _split3(c_ref[0, pl.ds(hh, 1), :])
    ri = lax.broadcasted_iota(jnp.int32, (LANES, tq), 0)
    j = ri % 8
    val = jnp.where(j < 3, 1.0, jnp.where(j == 3, hi, jnp.where(j == 4, mid, jnp.where(j == 5, lo, 0.0))))
    qa_ref[LANES:2 * LANES, :] = jnp.where(ri // 8 == hh, val, 0.0).astype(bf16)
    _flash_core(qa_ref, kn_ref, vt_ref, lambda k0: kx_ref[0, pl.ds(k0, tk), :], o_ref, q0, tq, tk)


def _moba_attn_kernel(q_ref, sel_ref, kn_ref, vt_ref, o_ref, qa_ref, *, tq, tk, nb):
    hh = pl.program_id(1)
    q0 = pl.multiple_of(pl.program_id(2) * tq, tq)
    qa_ref[0:LANES, :] = _q_rows(q_ref[0], hh % 2)
    qa_ref[LANES:2 * LANES, :] = jnp.concatenate(
        [sel_ref[0], jnp.zeros((LANES - nb, tq), bf16)], axis=0)

    def kx_fn(k0):
        kblk = (k0 + lax.broadcasted_iota(jnp.int32, (tk, LANES), 0)) // MOBA_BLOCK
        ln = lax.broadcasted_iota(jnp.int32, (tk, LANES), 1)
        return jnp.where(ln == kblk, 1.0, 0.0).astype(bf16)

    _flash_core(qa_ref, kn_ref, vt_ref, kx_fn, o_ref, q0, tq, tk)


def _prompt_attention(qt, extra, kn, kx, vtb, tq, tk, fox):
    B, _, S = qt.shape
    nb = S // MOBA_BLOCK
    assert nb <= LANES
    q_spec = pl.BlockSpec((1, HEAD_DIM, tq), lambda b, h, i: (b, h, i))
    kn_spec = pl.BlockSpec((1, S, LANES), lambda b, h, i: (b, 0, h // 2))
    vt_spec = pl.BlockSpec((1, HEAD_DIM, S), lambda b, h, i: (b, h, 0))
    if fox:
        kern = functools.partial(_fox_attn_kernel, tq=tq, tk=tk)
        in_specs = [q_spec, pl.BlockSpec((1, N_HEADS, tq), lambda b, h, i: (b, 0, i)), kn_spec,
                    pl.BlockSpec((1, S, LANES), lambda b, h, i: (b, 0, 0)), vt_spec]
        args = (qt, extra, kn, kx, vtb)
    else:
        kern = functools.partial(_moba_attn_kernel, tq=tq, tk=tk, nb=nb)
        in_specs = [q_spec, pl.BlockSpec((1, nb, tq), lambda b, h, i: (b, h, i)), kn_spec, vt_spec]
        args = (qt, extra, kn, vtb)
    return pl.pallas_call(
        kern, out_shape=jax.ShapeDtypeStruct((B, WIDTH, S), f32),
        grid=(B, N_HEADS, S // tq), in_specs=in_specs,
        out_specs=pl.BlockSpec((1, HEAD_DIM, tq), lambda b, h, i: (b, h, i)),
        scratch_shapes=[pltpu.VMEM((2 * LANES, tq), bf16)],
        compiler_params=_cparams(("arbitrary", "arbitrary", "arbitrary")),
        name="fox_attn" if fox else "moba_attn",
    )(*args)


def _merge_prompt_kernel(of_ref, om_ref, g_ref, x_ref, w_ref, fg_ref, y_ref, *, final):
    mix_t = jnp.concatenate([of_ref[0], om_ref[0]], axis=0) * g_ref[0]
    res = x_ref[0] + jnp.dot(mix_t.T.astype(bf16), w_ref[...], preferred_element_type=f32)
    y_ref[0] = _rmsnorm(res, fg_ref[...]) if final else res


def _merge_prompt(oft, omt, gt, x, w, fg, tm, final):
    B, S, D = x.shape
    tspec = lambda rows: pl.BlockSpec((1, rows, tm), lambda b, s: (b, 0, s))
    return pl.pallas_call(
        functools.partial(_merge_prompt_kernel, final=final),
        out_shape=jax.ShapeDtypeStruct((B, S, D), f32),
        grid=(B, S // tm),
        in_specs=[tspec(WIDTH), tspec(WIDTH), tspec(2 * WIDTH),
                  pl.BlockSpec((1, tm, D), lambda b, s: (b, s, 0)),
                  pl.BlockSpec((2 * WIDTH, D), lambda b, s: (0, 0)),
                  pl.BlockSpec((1, D), lambda b, s: (0, 0))],
        out_specs=pl.BlockSpec((1, tm, D), lambda b, s: (b, s, 0)),
        compiler_params=_cparams(("arbitrary", "arbitrary")),
        name="merge_prompt",
    )(oft, omt, gt, x, w, fg)


def _proj_dec_kernel(x_ref, g_ref, wt_ref, bfc_ref, bfr_ref, cos_ref, sin_ref,
                     qf_ref, kf_ref, vf_ref, logf_ref, dn_ref, dt_ref,
                     qm_ref, qmr_ref, km_ref, vm_ref, gate_ref, *, tm, dec_seq):
    h = _rmsnorm(x_ref[...], g_ref[...]).astype(bf16)

    def segn(lo, n):
        return lax.dot_general(h, wt_ref[lo:lo + n, :], _NT, preferred_element_type=f32)

    def segt(lo, n):
        return lax.dot_general(wt_ref[lo:lo + n, :], h, _NT, preferred_element_type=f32)

    qf_ref[...] = segn(OFF_QF, WIDTH) * QSCALE
    kf_ref[...] = segn(OFF_KF, WIDTH)
    vf_ref[...] = segn(OFF_VF, WIDTH)
    vm_ref[...] = segn(OFF_VM, WIDTH)
    gate_ref[...] = _silu(segn(OFF_G, 2 * WIDTH))

    logf_n = _log_sigmoid(segn(OFF_F, F_PAD)[:, 0:N_HEADS] + bfr_ref[...])
    logf_t = _log_sigmoid(segt(OFF_F, F_PAD)[0:N_HEADS] + bfc_ref[...])
    logf_ref[...] = logf_n
    r_i = lax.broadcasted_iota(jnp.int32, (tm, tm), 0)
    c_i = lax.broadcasted_iota(jnp.int32, (tm, tm), 1)
    same = r_i // dec_seq == c_i // dec_seq
    lower = jnp.where(jnp.logical_and(same, c_i <= r_i), 1.0, 0.0)
    upper = jnp.where(jnp.logical_and(same, r_i <= c_i), 1.0, 0.0)
    dn_ref[...] = jnp.dot(lower, logf_n, precision=_HI, preferred_element_type=f32) * LOG2E
    dt_ref[...] = jnp.dot(logf_t, upper, precision=_HI, preferred_element_type=f32) * LOG2E

    cos = cos_ref[...]
    sin = sin_ref[...]
    qm = _rope_rows(segt(OFF_QM, WIDTH), cos, sin).T
    qmr_ref[...] = qm
    qm_ref[...] = qm * QSCALE
    km_ref[...] = _rope_rows(segt(OFF_KM, WIDTH), cos, sin).T


def _proj_dec(x, g, wt, bfc, bfr, cos, sin, tm, dec_seq):
    T, D = x.shape
    nat = lambda cols: jax.ShapeDtypeStruct((T, cols), f32)
    nspec = lambda cols: pl.BlockSpec((tm, cols), lambda i: (i, 0))
    out_shape = [nat(WIDTH), nat(WIDTH), nat(WIDTH), nat(N_HEADS), nat(N_HEADS),
                 jax.ShapeDtypeStruct((N_HEADS, T), f32),
                 nat(WIDTH), nat(WIDTH), nat(WIDTH), nat(WIDTH), nat(2 * WIDTH)]
    out_specs = [nspec(WIDTH), nspec(WIDTH), nspec(WIDTH), nspec(N_HEADS), nspec(N_HEADS),
                 pl.BlockSpec((N_HEADS, tm), lambda i: (0, i)),
                 nspec(WIDTH), nspec(WIDTH), nspec(WIDTH), nspec(WIDTH), nspec(2 * WIDTH)]
    in_specs = [pl.BlockSpec((tm, D), lambda i: (i, 0)),
                pl.BlockSpec((1, D), lambda i: (0, 0)),
                pl.BlockSpec((WT_ROWS, D), lambda i: (0, 0)),
                pl.BlockSpec((N_HEADS, 1), lambda i: (0, 0)),
                pl.BlockSpec((1, N_HEADS), lambda i: (0, 0)),
                pl.BlockSpec((ROT_HALF, tm), lambda i: (0, i)),
                pl.BlockSpec((ROT_HALF, tm), lambda i: (0, i))]
    return pl.pallas_call(
        functools.partial(_proj_dec_kernel, tm=tm, dec_seq=dec_seq),
        out_shape=out_shape, grid=(T // tm,), in_specs=in_specs, out_specs=out_specs,
        compiler_params=_cparams(("arbitrary",)),
        name="proj_decode",
    )(x, g, wt, bfc, bfr, cos, sin)


def _fox_suffix_kernel(pt_ref, *refs, npg):
    page_refs, r_ref, carry_ref = refs[:npg], refs[npg], refs[npg + 1]

    @pl.when(pl.program_id(1) == 0)
    def _():
        carry_ref[...] = jnp.zeros_like(carry_ref)

    lane = lax.broadcasted_iota(jnp.int32, (N_HEADS, PAGE), 1)
    carry = carry_ref[...]
    for j in reversed(range(npg)):
        x = page_refs[j][0] * LOG2E
        p = x
        sh = 1
        while sh < PAGE:
            p = p + jnp.where(lane >= sh, pltpu.roll(p, sh, 1), 0.0)
            sh *= 2
        tot = jnp.broadcast_to(p[:, PAGE - 1:PAGE], (N_HEADS, PAGE))
        r_ref[0, :, j * PAGE:(j + 1) * PAGE] = (tot - p) + carry
        carry = carry + tot
    carry_ref[...] = carry


def _fox_suffix(page_table, logf_pool, npg):
    nseq, n_pages = page_table.shape
    nch = n_pages // npg

    def pmap(j):
        return lambda b, c, pt: (pt[b, (nch - 1 - c) * npg + j], 0, 0)

    grid_spec = pltpu.PrefetchScalarGridSpec(
        num_scalar_prefetch=1, grid=(nseq, nch),
        in_specs=[pl.BlockSpec((1, N_HEADS, PAGE), pmap(j)) for j in range(npg)],
        out_specs=pl.BlockSpec((1, N_HEADS, npg * PAGE), lambda b, c, pt: (b, 0, nch - 1 - c)),
        scratch_shapes=[pltpu.VMEM((N_HEADS, PAGE), f32)])
    return pl.pallas_call(
        functools.partial(_fox_suffix_kernel, npg=npg),
        out_shape=jax.ShapeDtypeStruct((nseq, N_HEADS, n_pages * PAGE), f32),
        grid_spec=grid_spec, compiler_params=_cparams(("arbitrary", "arbitrary")),
        name="fox_suffix",
    )(page_table, *([logf_pool] * npg))


def _moba_select_kernel(pt_ref, q_ref, *refs, npg, nbp):
    page_refs, sel_ref, kmt_ref = refs[:npg], refs[npg], refs[npg + 1]
    c = pl.program_id(1)
    ppb = MOBA_BLOCK // PAGE

    @pl.when(c == 0)
    def _():
        kmt_ref[...] = jnp.zeros_like(kmt_ref)

    lane3 = lax.broadcasted_iota(jnp.int32, (N_HEADS, HEAD_DIM, LANES), 2)
    for jb in range(npg // ppb):
        tot = page_refs[jb * ppb][0]
        for u in range(1, ppb):
            tot = tot + page_refs[jb * ppb + u][0]
        col = jnp.sum(tot, axis=-1, keepdims=True) * (1.0 / MOBA_BLOCK)
        n = c * (npg // ppb) + jb
        kmt_ref[...] = jnp.where(lane3 == n, col, kmt_ref[...])

    @pl.when(c == pl.num_programs(1) - 1)
    def _():
        nq = q_ref.shape[0]
        lane = lax.broadcasted_iota(jnp.int32, (nq, LANES), 1)
        for hh in range(N_HEADS):
            gs = jnp.dot(q_ref[:, hh * HEAD_DIM:(hh + 1) * HEAD_DIM], kmt_ref[hh],
                         precision=_HI, preferred_element_type=f32)
            sel = _top3_bias(jnp.where(lane < nbp, gs, -jnp.inf), lane, 1)
            sel_ref[0, hh] = jnp.where(sel, 0.0, NEG)


def _moba_select(page_table, qmr, k_pool, npg, dec_seq):
    nseq, n_pages = page_table.shape
    nch = n_pages // npg
    nbp = n_pages * PAGE // MOBA_BLOCK
    assert nbp <= LANES and npg % (MOBA_BLOCK // PAGE) == 0

    def pmap(j):
        return lambda b, c, pt: (pt[b, c * npg + j], 0, 0, 0)

    grid_spec = pltpu.PrefetchScalarGridSpec(
        num_scalar_prefetch=1, grid=(nseq, nch),
        in_specs=[pl.BlockSpec((dec_seq, WIDTH), lambda b, c, pt: (b, 0))]
        + [pl.BlockSpec((1, N_HEADS, HEAD_DIM, PAGE), pmap(j)) for j in range(npg)],
        out_specs=pl.BlockSpec((1, N_HEADS, dec_seq, LANES), lambda b, c, pt: (b, 0, 0, 0)),
        scratch_shapes=[pltpu.VMEM((N_HEADS, HEAD_DIM, LANES), f32)])
    return pl.pallas_call(
        functools.partial(_moba_select_kernel, npg=npg, nbp=nbp),
        out_shape=jax.ShapeDtypeStruct((nseq, N_HEADS, dec_seq, LANES), f32),
        grid_spec=grid_spec, compiler_params=_cparams(("arbitrary", "arbitrary")),
        name="moba_select",
    )(page_table, qmr, *([k_pool] * npg))


def _dec_attn_kernel(pt_ref, q_ref, kn_ref, vn_ref, *refs, npg, fox):
    if fox:
        r_ref, dn_ref, dt_ref = refs[:3]
        refs = refs[3:]
    else:
        sel_ref = refs[0]
        refs = refs[1:]
    k_refs, v_refs = refs[:npg], refs[npg:2 * npg]
    o_ref, m_ref, l_ref, acc_ref = refs[2 * npg:]
    c = pl.program_id(1)
    nq = q_ref.shape[0]
    pn = npg * PAGE
    hd = lambda ref, hh: ref[:, hh * HEAD_DIM:(hh + 1) * HEAD_DIM]

    @pl.when(c == 0)
    def _():
        qi = lax.broadcasted_iota(jnp.int32, (nq, nq), 0)
        ki = lax.broadcasted_iota(jnp.int32, (nq, nq), 1)
        for hh in range(N_HEADS):
            s = lax.dot_general(hd(q_ref, hh).astype(bf16), hd(kn_ref, hh).astype(bf16), _NT,
                                preferred_element_type=f32)
            if fox:
                s = s + (dn_ref[0][:, hh:hh + 1] - dt_ref[0][hh:hh + 1, :])
            s = jnp.where(ki <= qi, s, NEG)
            m0 = jnp.max(s, axis=-1, keepdims=True)
            p = jnp.exp2(s - m0)
            m_ref[hh] = jnp.broadcast_to(m0, (nq, LANES))
            l_ref[hh] = jnp.broadcast_to(jnp.sum(p, axis=-1, keepdims=True), (nq, LANES))
            acc_ref[hh] = jnp.dot(p.astype(bf16), hd(vn_ref, hh).astype(bf16),
                                  preferred_element_type=f32)

    if not fox:
        kblk = (c * pn + lax.broadcasted_iota(jnp.int32, (LANES, pn), 1)) // MOBA_BLOCK
        onehot = jnp.where(lax.broadcasted_iota(jnp.int32, (LANES, pn), 0) == kblk, 1.0, 0.0).astype(bf16)

    for hh in range(N_HEADS):
        kc = jnp.concatenate([k_refs[j][0, hh] for j in range(npg)], axis=1).astype(bf16)
        vc = jnp.concatenate([v_refs[j][0, hh] for j in range(npg)], axis=1).astype(bf16)
        qh = hd(q_ref, hh).astype(bf16)
        if fox:
            s = jnp.dot(qh, kc, preferred_element_type=f32)
            s = s + (r_ref[0, hh:hh + 1, :] + dn_ref[0][:, hh:hh + 1])
        else:
            s = jnp.dot(jnp.concatenate([sel_ref[0, hh].astype(bf16), qh], axis=1),
                        jnp.concatenate([onehot, kc], axis=0), preferred_element_type=f32)
        m_old = m_ref[hh][:, 0:1]
        m_new = jnp.maximum(m_old, jnp.max(s, axis=-1, keepdims=True))
        alpha = jnp.exp2(m_old - m_new)
        p = jnp.exp2(s - m_new)
        l_ref[hh] = jnp.broadcast_to(alpha * l_ref[hh][:, 0:1] + jnp.sum(p, axis=-1, keepdims=True),
                                     (nq, LANES))
        m_ref[hh] = jnp.broadcast_to(m_new, (nq, LANES))
        acc_ref[hh] = alpha * acc_ref[hh] + lax.dot_general(p.astype(bf16), vc, _NT,
                                                            preferred_element_type=f32)

    @pl.when(c == pl.num_programs(1) - 1)
    def _():
        o_ref[...] = jnp.concatenate(
            [acc_ref[hh] / l_ref[hh][:, 0:1] for hh in range(N_HEADS)], axis=1)


def _dec_attention(page_table, q, k_new, v_new, extras, k_pool, v_pool, npg, dec_seq, fox):
    nseq, n_pages = page_table.shape
    nch = n_pages // npg
    tok = pl.BlockSpec((dec_seq, WIDTH), lambda b, c, pt: (b, 0))

    def pmap(j):
        return lambda b, c, pt: (pt[b, c * npg + j], 0, 0, 0)

    if fox:
        sq = pl.BlockSpec((1, dec_seq, N_HEADS), lambda b, c, pt: (b, 0, 0))
        sq_t = pl.BlockSpec((1, N_HEADS, dec_seq), lambda b, c, pt: (b, 0, 0))
        extra_specs = [pl.BlockSpec((1, N_HEADS, npg * PAGE), lambda b, c, pt: (b, 0, c)), sq, sq_t]
    else:
        extra_specs = [pl.BlockSpec((1, N_HEADS, dec_seq, LANES), lambda b, c, pt: (b, 0, 0, 0))]
    page_spec = [pl.BlockSpec((1, N_HEADS, HEAD_DIM, PAGE), pmap(j)) for j in range(npg)]
    grid_spec = pltpu.PrefetchScalarGridSpec(
        num_scalar_prefetch=1, grid=(nseq, nch),
        in_specs=[tok, tok, tok] + extra_specs + page_spec + page_spec,
        out_specs=tok,
        scratch_shapes=[pltpu.VMEM((N_HEADS, dec_seq, LANES), f32),
                        pltpu.VMEM((N_HEADS, dec_seq, LANES), f32),
                        pltpu.VMEM((N_HEADS, dec_seq, HEAD_DIM), f32)])
    return pl.pallas_call(
        functools.partial(_dec_attn_kernel, npg=npg, fox=fox),
        out_shape=jax.ShapeDtypeStruct((nseq * dec_seq, WIDTH), f32),
        grid_spec=grid_spec, compiler_params=_cparams(("arbitrary", "arbitrary")),
        name="fox_decode" if fox else "moba_decode",
    )(page_table, q, k_new, v_new, *extras, *([k_pool] * npg), *([v_pool] * npg))


def _merge_dec_kernel(of_ref, om_ref, g_ref, x_ref, w_ref, fg_ref, y_ref, *, final):
    mix = jnp.concatenate([of_ref[...], om_ref[...]], axis=1) * g_ref[...]
    res = x_ref[...] + jnp.dot(mix.astype(bf16), w_ref[...], preferred_element_type=f32)
    y_ref[...] = _rmsnorm(res, fg_ref[...]) if final else res


def _merge_dec(of, om, g, x, w, fg, tm, final):
    T, D = x.shape
    nspec = lambda cols: pl.BlockSpec((tm, cols), lambda i: (i, 0))
    return pl.pallas_call(
        functools.partial(_merge_dec_kernel, final=final),
        out_shape=jax.ShapeDtypeStruct((T, D), f32), grid=(T // tm,),
        in_specs=[nspec(WIDTH), nspec(WIDTH), nspec(2 * WIDTH), nspec(D),
                  pl.BlockSpec((2 * WIDTH, D), lambda i: (0, 0)),
                  pl.BlockSpec((1, D), lambda i: (0, 0))],
        out_specs=nspec(D), compiler_params=_cparams(("arbitrary",)),
        name="merge_decode",
    )(of, om, g, x, w, fg)


def _prep_weight(w_in_l):
    wt = jnp.transpose(w_in_l)
    f0 = 3 * WIDTH
    return jnp.concatenate(
        [wt[:f0], wt[f0 + N_HEADS:], wt[f0:f0 + N_HEADS],
         jnp.zeros((F_PAD - N_HEADS, wt.shape[1]), wt.dtype)], axis=0).astype(bf16)


def _rope_tables(pos):
    inv_freq = ROPE_THETA ** (-(jnp.arange(ROT_HALF, dtype=f32) * 2.0 / (2 * ROT_HALF)))
    ang = inv_freq[:, None] * pos.astype(f32)[None, :]
    return jnp.cos(ang), jnp.sin(ang)


def _heads_out(t):
    B, _, S = t.shape
    return jnp.transpose(t.reshape(B, N_HEADS, HEAD_DIM, S), (0, 3, 1, 2))


def _pick(n, choices):
    for c in choices:
        if n % c == 0:
            return c
    raise ValueError(f"no tile size among {choices} divides {n}")


def kernel(x_prompt, x_sample, cache_fox_k, cache_fox_v, cache_fox_logf, cache_moba_k, cache_moba_v,
           page_table, norm_gain, w_in, b_forget, w_out, final_norm_gain):
    depth = w_in.shape[0]
    B, S, D = x_prompt.shape
    nseq, dec_seq, _ = x_sample.shape
    n_pages = page_table.shape[1]
    past_len = n_pages * PAGE
    assert w_in.shape[2] == 8 * WIDTH + N_HEADS and S % MOBA_BLOCK == 0
    assert past_len % MOBA_BLOCK == 0 and dec_seq <= MOBA_BLOCK

    tm = _pick(S, (512, 256))
    tq = _pick(S, (512, 256))
    tk = MOBA_BLOCK
    T = nseq * dec_seq
    tmd = _pick(T, (256, 128, 64, 32, 16, 8))
    npg = _pick(n_pages, (8, 4, 2))

    cos_p, sin_p = _rope_tables(jnp.arange(S))
    cos_s, sin_s = _rope_tables(past_len + jnp.arange(T) % dec_seq)
    fg = final_norm_gain.reshape(1, D)

    hp = x_prompt
    hs = x_sample.reshape(T, D)
    outs = [[] for _ in range(10)]
    for l in range(depth):
        wt = _prep_weight(w_in[l])
        wo = w_out[l].astype(bf16)
        g = norm_gain[l].reshape(1, D)
        bfc = b_forget[l].reshape(N_HEADS, 1)
        bfr = b_forget[l].reshape(1, N_HEADS)

        (qft, kft, vft, vftb, logft, ct, kfn, kxf,
         qmt, kmt, vmt, vmtb, kmn, selt, gt) = _proj_prompt(hp, g, wt, bfc, cos_p, sin_p, tm)
        oft = _prompt_attention(qft, ct, kfn, kxf, vftb, tq, tk, True)
        omt = _prompt_attention(qmt, selt, kmn, None, vmtb, tq, tk, False)
        hp = _merge_prompt(oft, omt, gt, hp, wo, fg, tm, l == depth - 1)
        outs[0].append(_heads_out(kft)); outs[1].append(_heads_out(vft))
        outs[2].append(jnp.transpose(logft, (0, 2, 1)))
        outs[3].append(_heads_out(kmt)); outs[4].append(_heads_out(vmt))

        (qfs, kfs, vfs, logfs, dn, dt, qms, qmrs, kms, vms, gs) = _proj_dec(
            hs, g, wt, bfc, bfr, cos_s, sin_s, tmd, dec_seq)
        to_pages = lambda pool: jnp.transpose(pool, (0, 2, 3, 1))
        r = _fox_suffix(page_table, jnp.transpose(cache_fox_logf[l], (0, 2, 1)), npg)
        dn3 = dn.reshape(nseq, dec_seq, N_HEADS)
        dt3 = jnp.transpose(dt.reshape(N_HEADS, nseq, dec_seq), (1, 0, 2))
        ofs = _dec_attention(page_table, qfs, kfs, vfs, (r, dn3, dt3),
                             to_pages(cache_fox_k[l]), to_pages(cache_fox_v[l]), npg, dec_seq, True)
        mk_pool = to_pages(cache_moba_k[l])
        selb = _moba_select(page_table, qmrs, mk_pool, npg, dec_seq)
        oms = _dec_attention(page_table, qms, kms, vms, (selb,),
                             mk_pool, to_pages(cache_moba_v[l]), npg, dec_seq, False)
        hs = _merge_dec(ofs, oms, gs, hs, wo, fg, tmd, l == depth - 1)
        heads = lambda t: t.reshape(nseq, dec_seq, N_HEADS, HEAD_DIM)
        outs[5].append(heads(kfs)); outs[6].append(heads(vfs))
        outs[7].append(logfs.reshape(nseq, dec_seq, N_HEADS))
        outs[8].append(heads(kms)); outs[9].append(heads(vms))

    return (hp, hs.reshape(nseq, dec_seq, D)) + tuple(jnp.stack(o) for o in outs)
```

```python
import functools

import jax
import jax.numpy as jnp
from jax import lax
from jax.experimental import pallas as pl
from jax.experimental.pallas import tpu as pltpu

f32 = jnp.float32
bf16 = jnp.bfloat16

HEAD_DIM = 64
N_HEADS = 8
WIDTH = N_HEADS * HEAD_DIM
ROT_HALF = 8
ROPE_THETA = 500000.0
PAGE = 128
MOBA_BLOCK = 256
MOBA_TOPK = 3
NORM_EPS = 1e-6
LOG2E = 1.4426950408889634
QSCALE = HEAD_DIM ** -0.5 * LOG2E
NEG = -1e30
LANES = 128
BF16_ROWS = 16
MXU_DIM = 256
VMEM_LIMIT = 56 * 1024 * 1024

OFF_QF, OFF_KF, OFF_VF = 0, WIDTH, 2 * WIDTH
OFF_QM, OFF_KM, OFF_VM = 3 * WIDTH, 4 * WIDTH, 5 * WIDTH
OFF_G = 6 * WIDTH
OFF_F = 6 * WIDTH + 2 * WIDTH
WT_ROWS = OFF_F + BF16_ROWS

_NT = (((1,), (1,)), ((), ()))
_HI = lax.Precision.HIGHEST


def _cparams(sem):
    return pltpu.CompilerParams(dimension_semantics=sem, vmem_limit_bytes=VMEM_LIMIT)


def _log_sigmoid(x):
    return jnp.minimum(x, 0.0) - jnp.log(1.0 + jnp.exp(-jnp.abs(x)))


def _silu(x):
    return x * (1.0 / (1.0 + jnp.exp(-x)))


def _rmsnorm(x, g):
    return x * lax.rsqrt(jnp.mean(x * x, axis=-1, keepdims=True) + NORM_EPS) * g


def _split3(x):
    hi = x.astype(bf16).astype(f32)
    r = x - hi
    mid = r.astype(bf16).astype(f32)
    lo = (r - mid).astype(bf16).astype(f32)
    return hi, mid, lo


def _rope_rows(z, cos, sin):
    t = z.shape[-1]
    z3 = z.reshape(N_HEADS, HEAD_DIM, t)
    x1 = z3[:, 0:ROT_HALF]
    x2 = z3[:, ROT_HALF:2 * ROT_HALF]
    out = jnp.concatenate([x1 * cos - x2 * sin, x2 * cos + x1 * sin, z3[:, 2 * ROT_HALF:]], axis=1)
    return out.reshape(N_HEADS * HEAD_DIM, t)


def _lane_prefix_sum(x):
    n = x.shape[-1]
    lane = lax.broadcasted_iota(jnp.int32, x.shape, x.ndim - 1)
    sh = 1
    while sh < n:
        x = x + jnp.where(lane >= sh, pltpu.roll(x, sh, x.ndim - 1), 0.0)
        sh *= 2
    return x


def _top3(gs, idx, axis):
    big = jnp.int32(2 ** 30)
    sel = jnp.zeros(gs.shape, jnp.bool_)
    for _ in range(MOBA_TOPK):
        mx = jnp.max(gs, axis=axis, keepdims=True)
        first = jnp.min(jnp.where(gs == mx, idx, big), axis=axis, keepdims=True)
        pick = jnp.logical_and(idx == first, mx > -jnp.inf)
        sel = jnp.logical_or(sel, pick)
        gs = jnp.where(pick, -jnp.inf, gs)
    return sel


def _proj_prompt_kernel(x_ref, g_ref, wt_ref, bf_ref, cos_ref, sin_ref,
                        qf_ref, kf_ref, vf_ref, vfb_ref, logf_ref, c_ref, kfn_ref, kxf_ref,
                        qm_ref, km_ref, vm_ref, vmb_ref, kmn_ref, sel_ref, gate_ref, qn_ref, kn_ref,
                        carry_ref, kmean_ref, *, tm, nb):
    s = pl.program_id(1)

    @pl.when(s == 0)
    def _():
        carry_ref[...] = jnp.zeros_like(carry_ref)
        kmean_ref[...] = jnp.zeros_like(kmean_ref)

    h = _rmsnorm(x_ref[0], g_ref[...]).astype(bf16)

    def seg(lo, n):
        return lax.dot_general(wt_ref[lo:lo + n, :], h, _NT, preferred_element_type=f32)

    def sq_norms(z):
        z3 = z.reshape(N_HEADS, HEAD_DIM, tm)
        return jnp.sum(z3 * z3, axis=1)

    qf = seg(OFF_QF, WIDTH) * QSCALE
    qf_ref[0] = qf.astype(bf16)
    qn_ref[0] = sq_norms(qf)
    kf = seg(OFF_KF, WIDTH)
    kf_ref[0] = kf
    kn_ref[0] = sq_norms(kf)
    kfn_ref[0] = kf.T.astype(bf16)
    vf = seg(OFF_VF, WIDTH)
    vf_ref[0] = vf
    vfb_ref[0] = vf.astype(bf16)

    logf = _log_sigmoid(seg(OFF_F, BF16_ROWS)[0:N_HEADS] + bf_ref[...])
    logf_ref[0] = logf
    c = _lane_prefix_sum(logf) + carry_ref[:, 0:1]
    carry_ref[...] = jnp.broadcast_to(c[:, tm - 1:tm], carry_ref.shape)
    c2 = c * LOG2E
    c_ref[0] = c2

    j8 = lax.broadcasted_iota(jnp.int32, (N_HEADS, tm), 0)
    blocks = []
    for hh in range(N_HEADS):
        hi, mid, lo = _split3(-c2[hh:hh + 1, :])
        blocks.append(jnp.where(j8 == 0, hi, jnp.where(j8 == 1, mid, jnp.where(j8 == 2, lo,
                      jnp.where(j8 < 6, 1.0, 0.0)))))
    blocks.append(jnp.zeros((LANES - N_HEADS * 8, tm), f32))
    kxf_ref[0] = jnp.concatenate(blocks, axis=0).T.astype(bf16)

    cos = cos_ref[...]
    sin = sin_ref[...]
    qm = _rope_rows(seg(OFF_QM, WIDTH), cos, sin)
    qm_ref[0] = (qm * QSCALE).astype(bf16)
    km = _rope_rows(seg(OFF_KM, WIDTH), cos, sin)
    km_ref[0] = km
    kmn = km.T
    kmn_ref[0] = kmn.astype(bf16)
    vm = seg(OFF_VM, WIDTH)
    vm_ref[0] = vm
    vmb_ref[0] = vm.astype(bf16)

    nblk_t = tm // MOBA_BLOCK
    for j in range(nblk_t):
        kmean_ref[pl.ds(s * nblk_t + j, 1), :] = jnp.mean(
            kmn[j * MOBA_BLOCK:(j + 1) * MOBA_BLOCK], axis=0, keepdims=True)
    km_all = kmean_ref[...]

    n_i = lax.broadcasted_iota(jnp.int32, (nb, tm), 0)
    blk = (s * tm + lax.broadcasted_iota(jnp.int32, (nb, tm), 1)) // MOBA_BLOCK
    valid = n_i < blk
    own = n_i == blk
    for hh in range(N_HEADS):
        gs = lax.dot_general(km_all[:, hh * HEAD_DIM:(hh + 1) * HEAD_DIM],
                             qm[hh * HEAD_DIM:(hh + 1) * HEAD_DIM],
                             (((1,), (0,)), ((), ())), precision=_HI, preferred_element_type=f32)
        sel = _top3(jnp.where(valid, gs, -jnp.inf), n_i, 0)
        bias = jnp.where(jnp.logical_or(sel, own), 0.0, NEG)
        sel_ref[0, hh * nb:(hh + 1) * nb, :] = bias.astype(bf16)

    gate_ref[0] = _silu(seg(OFF_G, 2 * WIDTH))


def _proj_prompt(x, g, wt, bfc, cos, sin, tm):
    B, S, D = x.shape
    nb = S // MOBA_BLOCK
    tok = lambda rows, dt: jax.ShapeDtypeStruct((B, rows, S), dt)
    nat = lambda cols, dt: jax.ShapeDtypeStruct((B, S, cols), dt)
    tspec = lambda rows: pl.BlockSpec((1, rows, tm), lambda b, s: (b, 0, s))
    nspec = lambda cols: pl.BlockSpec((1, tm, cols), lambda b, s: (b, s, 0))
    out_shape = [tok(WIDTH, bf16), tok(WIDTH, f32), tok(WIDTH, f32), tok(WIDTH, bf16),
                 tok(N_HEADS, f32), tok(N_HEADS, f32), nat(WIDTH, bf16), nat(LANES, bf16),
                 tok(WIDTH, bf16), tok(WIDTH, f32), tok(WIDTH, f32), tok(WIDTH, bf16),
                 nat(WIDTH, bf16), tok(N_HEADS * nb, bf16), tok(2 * WIDTH, f32),
                 tok(N_HEADS, f32), tok(N_HEADS, f32)]
    out_specs = [tspec(WIDTH), tspec(WIDTH), tspec(WIDTH), tspec(WIDTH),
                 tspec(N_HEADS), tspec(N_HEADS), nspec(WIDTH), nspec(LANES),
                 tspec(WIDTH), tspec(WIDTH), tspec(WIDTH), tspec(WIDTH),
                 nspec(WIDTH), tspec(N_HEADS * nb), tspec(2 * WIDTH),
                 tspec(N_HEADS), tspec(N_HEADS)]
    in_specs = [pl.BlockSpec((1, tm, D), lambda b, s: (b, s, 0)),
                pl.BlockSpec((1, D), lambda b, s: (0, 0)),
                pl.BlockSpec((WT_ROWS, D), lambda b, s: (0, 0)),
                pl.BlockSpec((N_HEADS, 1), lambda b, s: (0, 0)),
                pl.BlockSpec((ROT_HALF, tm), lambda b, s: (0, s)),
                pl.BlockSpec((ROT_HALF, tm), lambda b, s: (0, s))]
    return pl.pallas_call(
        functools.partial(_proj_prompt_kernel, tm=tm, nb=nb),
        out_shape=out_shape, grid=(B, S // tm), in_specs=in_specs, out_specs=out_specs,
        scratch_shapes=[pltpu.VMEM((N_HEADS, LANES), f32), pltpu.VMEM((nb, WIDTH), f32)],
        compiler_params=_cparams(("arbitrary", "arbitrary")),
        name="proj_prompt",
    )(x, g, wt, bfc, cos, sin)


def _flash_pair(qa_ref, kn_ref, vt_ref, kx_fn, o_ref, st_ref, q0, tq, tk, lo2):
    tg = MXU_DIM
    assert tk == tg and tq == 2 * tg
    chains = [(hh, g) for hh in range(2) for g in range(2)]
    every = list(range(len(chains)))
    ones = jnp.ones((BF16_ROWS, tk), bf16)

    def scores(k0, slot, idxs):
        ka = jnp.concatenate([kn_ref[0, pl.ds(k0, tk), :], kx_fn(k0)], axis=1)
        for i in idxs:
            hh, g = chains[i]
            st_ref[slot, i] = jnp.dot(ka, qa_ref[hh, :, g * tg:(g + 1) * tg], preferred_element_type=f32)

    def chain_step(k0, slot, i, st8, masked):
        hh, g = chains[i]
        m, acc = st8
        if masked:
            kpos = k0 + lax.broadcasted_iota(jnp.int32, (tk, tg), 0)
            qpos = q0 + g * tg + lax.broadcasted_iota(jnp.int32, (tk, tg), 1)
            st_ref[slot, i] = jnp.where(kpos <= qpos, st_ref[slot, i], NEG)
        m_new = jnp.maximum(m, jnp.max(st_ref[slot, i], axis=0, keepdims=True))
        p = jnp.exp2(st_ref[slot, i] - m_new).astype(bf16)
        va = jnp.concatenate([vt_ref[0, hh * HEAD_DIM:(hh + 1) * HEAD_DIM, pl.ds(k0, tk)], ones], axis=0)
        return m_new, jnp.exp2(m - m_new) * acc + jnp.dot(va, p, preferred_element_type=f32)

    def consume(k0, slot, state, todo):
        return tuple(chain_step(k0, slot, i, state[i], todo[i]) if i in todo else state[i] for i in every)

    unmasked = {i: False for i in every}
    n_low2 = q0 // (2 * tk)
    state = tuple((jnp.full((1, tg), NEG, f32), jnp.zeros((HEAD_DIM + BF16_ROWS, tg), f32))
                  for _ in chains)

    def body(j, state):
        k0 = pl.multiple_of(2 * j * tk, 2 * tk)
        scores(k0 + tk, 1, every)
        state = consume(k0, 0, state, unmasked)
        scores(k0 + 2 * tk, 0, every)
        return consume(k0 + tk, 1, state, unmasked)

    scores(pl.multiple_of(2 * lo2 * tk, 2 * tk), 0, every)
    state = lax.fori_loop(lo2, n_low2, body, state)
    upper = [i for i in every if chains[i][1] == 1]
    scores(q0 + tk, 1, upper)
    state = consume(q0, 0, state, {i: chains[i][1] == 0 for i in every})
    state = consume(q0 + tk, 1, state, {i: True for i in upper})
    for (hh, g), (_, acc) in zip(chains, state):
        o_ref[0, hh * HEAD_DIM:(hh + 1) * HEAD_DIM, g * tg:(g + 1) * tg] = (
            acc[0:HEAD_DIM] / acc[HEAD_DIM:HEAD_DIM + 1])


def _fox_attn_kernel(lo_ref, q_ref, c_ref, kn_ref, kx_ref, vt_ref, o_ref, qa_ref, st_ref, *, tq, tk):
    pair = pl.program_id(1)
    q0 = pl.multiple_of(pl.program_id(2) * tq, tq)
    ri = lax.broadcasted_iota(jnp.int32, (LANES, tq), 0)
    j = ri % 8
    for hh in range(2):
        qa_ref[hh, 0:LANES, :] = jnp.zeros((LANES, tq), bf16)
        qa_ref[hh, hh * HEAD_DIM:(hh + 1) * HEAD_DIM, :] = q_ref[0, hh * HEAD_DIM:(hh + 1) * HEAD_DIM, :]
        head = 2 * pair + hh
        hi, mid, lo = _split3(c_ref[0, pl.ds(head, 1), :])
        val = jnp.where(j < 3, 1.0, jnp.where(j == 3, hi, jnp.where(j == 4, mid, jnp.where(j == 5, lo, 0.0))))
        qa_ref[hh, LANES:2 * LANES, :] = jnp.where(ri // 8 == head, val, 0.0).astype(bf16)
    _flash_pair(qa_ref, kn_ref, vt_ref, lambda k0: kx_ref[0, pl.ds(k0, tk), :], o_ref, st_ref,
                q0, tq, tk, _first_pair(lo_ref))


def _first_pair(lo_ref):
    b, p, i = pl.program_id(0), pl.program_id(1), pl.program_id(2)
    return lo_ref[(b * pl.num_programs(1) + p) * pl.num_programs(2) + i]


def _moba_attn_kernel(lo_ref, q_ref, sel_ref, kn_ref, vt_ref, o_ref, qa_ref, st_ref, *, tq, tk, nb):
    q0 = pl.multiple_of(pl.program_id(2) * tq, tq)
    for hh in range(2):
        qa_ref[hh, 0:LANES, :] = jnp.zeros((LANES, tq), bf16)
        qa_ref[hh, hh * HEAD_DIM:(hh + 1) * HEAD_DIM, :] = q_ref[0, hh * HEAD_DIM:(hh + 1) * HEAD_DIM, :]
        qa_ref[hh, LANES:2 * LANES, :] = jnp.concatenate(
            [sel_ref[0, hh * nb:(hh + 1) * nb, :], jnp.zeros((LANES - nb, tq), bf16)], axis=0)

    assert tk == MOBA_BLOCK

    def kx_fn(k0):
        ln = lax.broadcasted_iota(jnp.int32, (tk, LANES), 1)
        return jnp.where(ln == k0 // MOBA_BLOCK, 1.0, 0.0).astype(bf16)

    _flash_pair(qa_ref, kn_ref, vt_ref, kx_fn, o_ref, st_ref, q0, tq, tk, _first_pair(lo_ref))


def _prompt_attention(lo2, qt, extra, kn, kx, vtb, tq, tk, fox):
    B, _, S = qt.shape
    nb = S // MOBA_BLOCK
    assert nb <= LANES
    pair_spec = pl.BlockSpec((1, 2 * HEAD_DIM, tq), lambda b, p, i, lo: (b, p, i))
    kn_spec = pl.BlockSpec((1, S, LANES), lambda b, p, i, lo: (b, 0, p))
    vt_spec = pl.BlockSpec((1, 2 * HEAD_DIM, S), lambda b, p, i, lo: (b, p, 0))
    if fox:
        kern = functools.partial(_fox_attn_kernel, tq=tq, tk=tk)
        in_specs = [pair_spec, pl.BlockSpec((1, N_HEADS, tq), lambda b, p, i, lo: (b, 0, i)), kn_spec,
                    pl.BlockSpec((1, S, LANES), lambda b, p, i, lo: (b, 0, 0)), vt_spec]
        args = (qt, extra, kn, kx, vtb)
    else:
        kern = functools.partial(_moba_attn_kernel, tq=tq, tk=tk, nb=nb)
        in_specs = [pair_spec, pl.BlockSpec((1, 2 * nb, tq), lambda b, p, i, lo: (b, p, i)), kn_spec, vt_spec]
        args = (qt, extra, kn, vtb)
    grid_spec = pltpu.PrefetchScalarGridSpec(
        num_scalar_prefetch=1, grid=(B, N_HEADS // 2, S // tq), in_specs=in_specs, out_specs=pair_spec,
        scratch_shapes=[pltpu.VMEM((2, 2 * LANES, tq), bf16),
                        pltpu.VMEM((2, 4, tk, MXU_DIM), f32)])
    return pl.pallas_call(
        kern, out_shape=jax.ShapeDtypeStruct((B, WIDTH, S), f32), grid_spec=grid_spec,
        compiler_params=_cparams(("arbitrary", "arbitrary", "arbitrary")),
        name="fox_attn" if fox else "moba_attn",
    )(lo2, *args)


def _merge_prompt_kernel(of_ref, om_ref, g_ref, x_ref, w_ref, fg_ref, y_ref, *, final):
    mix_t = jnp.concatenate([of_ref[0], om_ref[0]], axis=0) * g_ref[0]
    res = x_ref[0] + jnp.dot(mix_t.T.astype(bf16), w_ref[...], preferred_element_type=f32)
    y_ref[0] = _rmsnorm(res, fg_ref[...]) if final else res


def _merge_prompt(oft, omt, gt, x, w, fg, tm, final):
    B, S, D = x.shape
    tspec = lambda rows: pl.BlockSpec((1, rows, tm), lambda b, s: (b, 0, s))
    return pl.pallas_call(
        functools.partial(_merge_prompt_kernel, final=final),
        out_shape=jax.ShapeDtypeStruct((B, S, D), f32),
        grid=(B, S // tm),
        in_specs=[tspec(WIDTH), tspec(WIDTH), tspec(2 * WIDTH),
                  pl.BlockSpec((1, tm, D), lambda b, s: (b, s, 0)),
                  pl.BlockSpec((2 * WIDTH, D), lambda b, s: (0, 0)),
                  pl.BlockSpec((1, D), lambda b, s: (0, 0))],
        out_specs=pl.BlockSpec((1, tm, D), lambda b, s: (b, s, 0)),
        compiler_params=_cparams(("arbitrary", "arbitrary")),
        name="merge_prompt",
    )(oft, omt, gt, x, w, fg)


def _proj_dec_kernel(x_ref, g_ref, wt_ref, bfc_ref, bfr_ref, cos_ref, sin_ref,
                     qf_ref, kf_ref, vf_ref, logf_ref, dn_ref, dt_ref,
                     qm_ref, qmr_ref, km_ref, vm_ref, gate_ref, *, tm, dec_seq):
    h = _rmsnorm(x_ref[...], g_ref[...]).astype(bf16)

    def segn(lo, n):
        return lax.dot_general(h, wt_ref[lo:lo + n, :], _NT, preferred_element_type=f32)

    def segt(lo, n):
        return lax.dot_general(wt_ref[lo:lo + n, :], h, _NT, preferred_element_type=f32)

    qf_ref[...] = segn(OFF_QF, WIDTH) * QSCALE
    kf_ref[...] = segn(OFF_KF, WIDTH)
    vf_ref[...] = segn(OFF_VF, WIDTH)
    vm_ref[...] = segn(OFF_VM, WIDTH)
    gate_ref[...] = _silu(segn(OFF_G, 2 * WIDTH))

    logf_n = _log_sigmoid(segn(OFF_F, BF16_ROWS)[:, 0:N_HEADS] + bfr_ref[...])
    logf_t = _log_sigmoid(segt(OFF_F, BF16_ROWS)[0:N_HEADS] + bfc_ref[...])
    logf_ref[...] = logf_n
    r_i = lax.broadcasted_iota(jnp.int32, (tm, tm), 0)
    c_i = lax.broadcasted_iota(jnp.int32, (tm, tm), 1)
    same = r_i // dec_seq == c_i // dec_seq
    lower = jnp.where(jnp.logical_and(same, c_i <= r_i), 1.0, 0.0)
    upper = jnp.where(jnp.logical_and(same, r_i <= c_i), 1.0, 0.0)
    dn_ref[...] = jnp.dot(lower, logf_n, precision=_HI, preferred_element_type=f32) * LOG2E
    dt_ref[...] = jnp.dot(logf_t, upper, precision=_HI, preferred_element_type=f32) * LOG2E

    cos = cos_ref[...]
    sin = sin_ref[...]
    qm = _rope_rows(segt(OFF_QM, WIDTH), cos, sin).T
    qmr_ref[...] = qm
    qm_ref[...] = qm * QSCALE
    km_ref[...] = _rope_rows(segt(OFF_KM, WIDTH), cos, sin).T


def _proj_dec(x, g, wt, bfc, bfr, cos, sin, tm, dec_seq):
    T, D = x.shape
    nat = lambda cols: jax.ShapeDtypeStruct((T, cols), f32)
    nspec = lambda cols: pl.BlockSpec((tm, cols), lambda i: (i, 0))
    out_shape = [nat(WIDTH), nat(WIDTH), nat(WIDTH), nat(N_HEADS), nat(N_HEADS),
                 jax.ShapeDtypeStruct((N_HEADS, T), f32),
                 nat(WIDTH), nat(WIDTH), nat(WIDTH), nat(WIDTH), nat(2 * WIDTH)]
    out_specs = [nspec(WIDTH), nspec(WIDTH), nspec(WIDTH), nspec(N_HEADS), nspec(N_HEADS),
                 pl.BlockSpec((N_HEADS, tm), lambda i: (0, i)),
                 nspec(WIDTH), nspec(WIDTH), nspec(WIDTH), nspec(WIDTH), nspec(2 * WIDTH)]
    in_specs = [pl.BlockSpec((tm, D), lambda i: (i, 0)),
                pl.BlockSpec((1, D), lambda i: (0, 0)),
                pl.BlockSpec((WT_ROWS, D), lambda i: (0, 0)),
                pl.BlockSpec((N_HEADS, 1), lambda i: (0, 0)),
                pl.BlockSpec((1, N_HEADS), lambda i: (0, 0)),
                pl.BlockSpec((ROT_HALF, tm), lambda i: (0, i)),
                pl.BlockSpec((ROT_HALF, tm), lambda i: (0, i))]
    return pl.pallas_call(
        functools.partial(_proj_dec_kernel, tm=tm, dec_seq=dec_seq),
        out_shape=out_shape, grid=(T // tm,), in_specs=in_specs, out_specs=out_specs,
        compiler_params=_cparams(("arbitrary",)),
        name="proj_decode",
    )(x, g, wt, bfc, bfr, cos, sin)


def _moba_select_kernel(pt_ref, q_ref, *refs, npg, nbp):
    page_refs, sel_ref, kmt_ref = refs[:npg], refs[npg], refs[npg + 1]
    c = pl.program_id(1)
    ppb = MOBA_BLOCK // PAGE

    @pl.when(c == 0)
    def _():
        kmt_ref[...] = jnp.zeros_like(kmt_ref)

    lane3 = lax.broadcasted_iota(jnp.int32, (N_HEADS, HEAD_DIM, LANES), 2)
    cols = []
    for jb in range(npg // ppb):
        tot = page_refs[jb * ppb][0]
        for u in range(1, ppb):
            tot = tot + page_refs[jb * ppb + u][0]
        cols.append(jnp.sum(tot, axis=-1, keepdims=True) * (1.0 / MOBA_BLOCK))
    kmt = kmt_ref[...]
    for jb, col in enumerate(cols):
        kmt = jnp.where(lane3 == c * (npg // ppb) + jb, col, kmt)
    kmt_ref[...] = kmt

    @pl.when(c == pl.num_programs(1) - 1)
    def _():
        gs = jnp.einsum('hqd,hdn->hqn', q_ref[0], kmt, precision=_HI,
                        preferred_element_type=f32)
        lane = lax.broadcasted_iota(jnp.int32, gs.shape, 2)
        sel = _top3(jnp.where(lane < nbp, gs, -jnp.inf), lane, 2)
        sel_ref[0] = jnp.where(sel, 0.0, NEG)


def _moba_select(page_table, qmr4, k_pool, npg):
    nseq, n_pages = page_table.shape
    dec_seq = qmr4.shape[2]
    nch = n_pages // npg
    nbp = n_pages * PAGE // MOBA_BLOCK
    assert nbp <= LANES and npg % (MOBA_BLOCK // PAGE) == 0

    def pmap(j):
        return lambda b, c, pt: (pt[b, c * npg + j], 0, 0, 0)

    grid_spec = pltpu.PrefetchScalarGridSpec(
        num_scalar_prefetch=1, grid=(nseq, nch),
        in_specs=[pl.BlockSpec((1, N_HEADS, dec_seq, HEAD_DIM), lambda b, c, pt: (b, 0, 0, 0))]
        + [pl.BlockSpec((1, N_HEADS, HEAD_DIM, PAGE), pmap(j)) for j in range(npg)],
        out_specs=pl.BlockSpec((1, N_HEADS, dec_seq, LANES), lambda b, c, pt: (b, 0, 0, 0)),
        scratch_shapes=[pltpu.VMEM((N_HEADS, HEAD_DIM, LANES), f32)])
    return pl.pallas_call(
        functools.partial(_moba_select_kernel, npg=npg, nbp=nbp),
        out_shape=jax.ShapeDtypeStruct((nseq, N_HEADS, dec_seq, LANES), f32),
        grid_spec=grid_spec, compiler_params=_cparams(("arbitrary", "arbitrary")),
        name="moba_select",
    )(page_table, qmr4, *([k_pool] * npg))


def _dec_attn_kernel(pt_ref, q_ref, kn_ref, vn_ref, *refs, npg, fox):
    if fox:
        dq_ref, dk_ref = refs[:2]
        lf_refs = refs[2:2 + npg]
        refs = refs[2 + npg:]
    else:
        sel_ref = refs[0]
        refs = refs[1:]
    k_refs, v_refs = refs[:npg], refs[npg:2 * npg]
    o_ref, m_ref, l_ref, acc_ref = refs[2 * npg:2 * npg + 4]
    carry_ref = refs[2 * npg + 4] if fox else None
    c = pl.program_id(1)
    nq = q_ref.shape[2]
    pn = npg * PAGE
    q = q_ref[0].astype(bf16)

    @pl.when(c == 0)
    def _():
        s = jnp.einsum('hqd,hkd->hqk', q, kn_ref[0].astype(bf16), preferred_element_type=f32)
        if fox:
            s = s + (dq_ref[0] - dk_ref[0])
            carry_ref[...] = jnp.zeros_like(carry_ref)
        qi = lax.broadcasted_iota(jnp.int32, s.shape, 1)
        ki = lax.broadcasted_iota(jnp.int32, s.shape, 2)
        s = jnp.where(ki <= qi, s, NEG)
        m0 = jnp.max(s, axis=-1, keepdims=True)
        p = jnp.exp2(s - m0)
        m_ref[...] = jnp.broadcast_to(m0, m_ref.shape)
        l_ref[...] = jnp.broadcast_to(jnp.sum(p, axis=-1, keepdims=True), l_ref.shape)
        acc_ref[...] = jnp.einsum('hqk,hkd->hqd', p.astype(bf16), vn_ref[0].astype(bf16),
                                  preferred_element_type=f32)

    kc = jnp.concatenate([k_refs[j][0] for j in range(npg)], axis=-1).astype(bf16)
    vc = jnp.concatenate([v_refs[j][0] for j in range(npg)], axis=-1).astype(bf16)
    s = jnp.einsum('hqd,hdk->hqk', q, kc, preferred_element_type=f32)
    if fox:
        x = jnp.stack([lf_refs[j][0] for j in range(npg)], axis=0) * LOG2E
        pre = _lane_prefix_sum(x)
        tot = jnp.broadcast_to(pre[:, :, PAGE - 1:PAGE], pre.shape)
        carry = carry_ref[...]
        r = [None] * npg
        for j in reversed(range(npg)):
            r[j] = (tot[j] - pre[j]) + carry
            carry = carry + tot[j]
        carry_ref[...] = carry
        s = s + (jnp.concatenate(r, axis=-1)[:, None, :] + dq_ref[0])
    else:
        kblk = (c * pn + lax.broadcasted_iota(jnp.int32, (LANES, pn), 1)) // MOBA_BLOCK
        onehot = jnp.where(lax.broadcasted_iota(jnp.int32, (LANES, pn), 0) == kblk, 1.0, 0.0).astype(bf16)
        bias = jnp.dot(sel_ref[0].reshape(N_HEADS * nq, LANES).astype(bf16), onehot,
                       preferred_element_type=f32)
        s = s + bias.reshape(N_HEADS, nq, pn)
    m_old = m_ref[...][:, :, 0:1]
    m_new = jnp.maximum(m_old, jnp.max(s, axis=-1, keepdims=True))
    alpha = jnp.exp2(m_old - m_new)
    p = jnp.exp2(s - m_new)
    l_new = alpha * l_ref[...][:, :, 0:1] + jnp.sum(p, axis=-1, keepdims=True)
    acc = alpha * acc_ref[...] + jnp.einsum('hqk,hdk->hqd', p.astype(bf16), vc,
                                            preferred_element_type=f32)
    m_ref[...] = jnp.broadcast_to(m_new, m_ref.shape)
    l_ref[...] = jnp.broadcast_to(l_new, l_ref.shape)
    acc_ref[...] = acc

    @pl.when(c == pl.num_programs(1) - 1)
    def _():
        o_ref[0] = acc / l_new


def _dec_attention(page_table, q4, k_new4, v_new4, extras, lf_pool, k_pool, v_pool, npg, fox):
    nseq, n_pages = page_table.shape
    dec_seq = q4.shape[2]
    nch = n_pages // npg
    tok = pl.BlockSpec((1, N_HEADS, dec_seq, HEAD_DIM), lambda b, c, pt: (b, 0, 0, 0))
    chunk = (lambda c: nch - 1 - c) if fox else (lambda c: c)

    def pmap(j, nd):
        return lambda b, c, pt: (pt[b, chunk(c) * npg + j],) + (0,) * nd

    page_spec = [pl.BlockSpec((1, N_HEADS, HEAD_DIM, PAGE), pmap(j, 3)) for j in range(npg)]
    scratch = [pltpu.VMEM((N_HEADS, dec_seq, LANES), f32), pltpu.VMEM((N_HEADS, dec_seq, LANES), f32),
               pltpu.VMEM((N_HEADS, dec_seq, HEAD_DIM), f32)]
    if fox:
        extra_specs = [pl.BlockSpec((1, N_HEADS, dec_seq, 1), lambda b, c, pt: (b, 0, 0, 0)),
                       pl.BlockSpec((1, N_HEADS, 1, dec_seq), lambda b, c, pt: (b, 0, 0, 0))]
        extra_specs += [pl.BlockSpec((1, N_HEADS, PAGE), pmap(j, 2)) for j in range(npg)]
        extras = tuple(extras) + (lf_pool,) * npg
        scratch.append(pltpu.VMEM((N_HEADS, PAGE), f32))
    else:
        extra_specs = [pl.BlockSpec((1, N_HEADS, dec_seq, LANES), lambda b, c, pt: (b, 0, 0, 0))]
    grid_spec = pltpu.PrefetchScalarGridSpec(
        num_scalar_prefetch=1, grid=(nseq, nch),
        in_specs=[tok, tok, tok] + extra_specs + page_spec + page_spec,
        out_specs=tok, scratch_shapes=scratch)
    return pl.pallas_call(
        functools.partial(_dec_attn_kernel, npg=npg, fox=fox),
        out_shape=jax.ShapeDtypeStruct((nseq, N_HEADS, dec_seq, HEAD_DIM), f32),
        grid_spec=grid_spec, compiler_params=_cparams(("arbitrary", "arbitrary")),
        name="fox_decode" if fox else "moba_decode",
    )(page_table, q4, k_new4, v_new4, *extras, *([k_pool] * npg), *([v_pool] * npg))


def _merge_dec_kernel(of_ref, om_ref, g_ref, x_ref, w_ref, fg_ref, y_ref, *, final):
    mix = jnp.concatenate([of_ref[...], om_ref[...]], axis=1) * g_ref[...]
    res = x_ref[...] + jnp.dot(mix.astype(bf16), w_ref[...], preferred_element_type=f32)
    y_ref[...] = _rmsnorm(res, fg_ref[...]) if final else res


def _merge_dec(of, om, g, x, w, fg, tm, final):
    T, D = x.shape
    nspec = lambda cols: pl.BlockSpec((tm, cols), lambda i: (i, 0))
    return pl.pallas_call(
        functools.partial(_merge_dec_kernel, final=final),
        out_shape=jax.ShapeDtypeStruct((T, D), f32), grid=(T // tm,),
        in_specs=[nspec(WIDTH), nspec(WIDTH), nspec(2 * WIDTH), nspec(D),
                  pl.BlockSpec((2 * WIDTH, D), lambda i: (0, 0)),
                  pl.BlockSpec((1, D), lambda i: (0, 0))],
        out_specs=nspec(D), compiler_params=_cparams(("arbitrary",)),
        name="merge_decode",
    )(of, om, g, x, w, fg)


def _prep_weight(w_in_l):
    wt = jnp.transpose(w_in_l)
    f0 = 3 * WIDTH
    return jnp.concatenate(
        [wt[:f0], wt[f0 + N_HEADS:], wt[f0:f0 + N_HEADS],
         jnp.zeros((BF16_ROWS - N_HEADS, wt.shape[1]), wt.dtype)], axis=0).astype(bf16)


def _rope_tables(pos):
    inv_freq = ROPE_THETA ** (-(jnp.arange(ROT_HALF, dtype=f32) * 2.0 / (2 * ROT_HALF)))
    ang = inv_freq[:, None] * pos.astype(f32)[None, :]
    return jnp.cos(ang), jnp.sin(ang)


def _fox_first_pairs(c2, qn2, kn2, tq):
    B, H, S = c2.shape
    nqt = S // tq
    EXP2_ZERO = 160.0
    a = 1.02 * jnp.sqrt(jnp.max(qn2, axis=-1) * jnp.max(kn2, axis=-1)) + 1.0
    c_first = c2[:, :, ::tq]
    c_last = c2[:, :, tq - 1::tq]
    gap = 2.0 * a[:, :, None, None] + c_first[:, :, :, None] - c_last[:, :, None, :]
    below = jnp.arange(nqt)[None, :] < jnp.arange(nqt)[:, None]
    dead = jnp.logical_and(gap <= -EXP2_ZERO, below[None, None])
    dead = jnp.logical_and(dead[:, 0::2], dead[:, 1::2])
    first = jnp.sum(jnp.cumprod(dead.astype(jnp.int32), axis=-1), axis=-1)
    return first.reshape(-1).astype(jnp.int32)


def _heads_out(t):
    B, _, S = t.shape
    return jnp.transpose(t.reshape(B, N_HEADS, HEAD_DIM, S), (0, 3, 1, 2))


def _pick(n, choices):
    for c in choices:
        if n % c == 0:
            return c
    raise ValueError(f"no tile size among {choices} divides {n}")


def kernel(x_prompt, x_sample, cache_fox_k, cache_fox_v, cache_fox_logf, cache_moba_k, cache_moba_v,
           page_table, norm_gain, w_in, b_forget, w_out, final_norm_gain):
    depth = w_in.shape[0]
    B, S, D = x_prompt.shape
    nseq, dec_seq, _ = x_sample.shape
    n_pages = page_table.shape[1]
    past_len = n_pages * PAGE
    assert w_in.shape[2] == 8 * WIDTH + N_HEADS and S % MOBA_BLOCK == 0
    assert past_len % MOBA_BLOCK == 0 and dec_seq <= MOBA_BLOCK

    tm = _pick(S, (512, 256))
    tq = _pick(S, (512, 256))
    tk = MOBA_BLOCK
    T = nseq * dec_seq
    tmd = _pick(T, (256, 128, 64, 32, 16, 8))
    npg = _pick(n_pages, (16, 8, 4, 2))

    cos_p, sin_p = _rope_tables(jnp.arange(S))
    cos_s, sin_s = _rope_tables(past_len + jnp.arange(T) % dec_seq)
    fg = final_norm_gain.reshape(1, D)

    def by_head(t):
        return jnp.transpose(t.reshape(nseq, dec_seq, N_HEADS, HEAD_DIM), (0, 2, 1, 3))

    hp = x_prompt
    hs = x_sample.reshape(T, D)
    outs = [[] for _ in range(10)]
    for l in range(depth):
        wt = _prep_weight(w_in[l])
        wo = w_out[l].astype(bf16)
        g = norm_gain[l].reshape(1, D)
        bfc = b_forget[l].reshape(N_HEADS, 1)
        bfr = b_forget[l].reshape(1, N_HEADS)
        final = l == depth - 1

        (qft, kft, vft, vftb, logft, ct, kfn, kxf,
         qmt, kmt, vmt, vmtb, kmn, selt, gt, qn2, kn2) = _proj_prompt(hp, g, wt, bfc, cos_p, sin_p, tm)
        no_skip = jnp.zeros((B * (N_HEADS // 2) * (S // tq),), jnp.int32)
        oft = _prompt_attention(_fox_first_pairs(ct, qn2, kn2, tq), qft, ct, kfn, kxf, vftb, tq, tk, True)
        omt = _prompt_attention(no_skip, qmt, selt, kmn, None, vmtb, tq, tk, False)
        hp = _merge_prompt(oft, omt, gt, hp, wo, fg, tm, final)
        outs[0].append(_heads_out(kft)); outs[1].append(_heads_out(vft))
        outs[2].append(jnp.transpose(logft, (0, 2, 1)))
        outs[3].append(_heads_out(kmt)); outs[4].append(_heads_out(vmt))

        (qfs, kfs, vfs, logfs, dn, dt, qms, qmrs, kms, vms, gs) = _proj_dec(
            hs, g, wt, bfc, bfr, cos_s, sin_s, tmd, dec_seq)
        to_pages = lambda pool: jnp.transpose(pool, (0, 2, 3, 1))
        dq = jnp.transpose(dn.reshape(nseq, dec_seq, N_HEADS), (0, 2, 1))[..., None]
        dk = jnp.transpose(dt.reshape(N_HEADS, nseq, dec_seq), (1, 0, 2))[:, :, None, :]
        ofs = _dec_attention(page_table, by_head(qfs), by_head(kfs), by_head(vfs), (dq, dk),
                             jnp.transpose(cache_fox_logf[l], (0, 2, 1)),
                             to_pages(cache_fox_k[l]), to_pages(cache_fox_v[l]), npg, True)
        mk_pool = to_pages(cache_moba_k[l])
        selb = _moba_select(page_table, by_head(qmrs), mk_pool, npg)
        oms = _dec_attention(page_table, by_head(qms), by_head(kms), by_head(vms), (selb,),
                             None, mk_pool, to_pages(cache_moba_v[l]), npg, False)
        flat = lambda o4: jnp.transpose(o4, (0, 2, 1, 3)).reshape(T, WIDTH)
        hs = _merge_dec(flat(ofs), flat(oms), gs, hs, wo, fg, tmd, final)
        heads = lambda t: t.reshape(nseq, dec_seq, N_HEADS, HEAD_DIM)
        outs[5].append(heads(kfs)); outs[6].append(heads(vfs))
        outs[7].append(logfs.reshape(nseq, dec_seq, N_HEADS))
        outs[8].append(heads(kms)); outs[9].append(heads(vms))

    return (hp, hs.reshape(nseq, dec_seq, D)) + tuple(jnp.stack(o) for o in outs)
```
